```python
import math
import jax, jax.numpy as jnp
from jax import lax
import numpy as np

D_MODEL = 1024
BATCH = 2
SEQ = 16384
DEPTH = 4

N_MIXERS = 2
N_SUB = 3
D_FF = 2816
D_RNN = D_MODEL
RG_HEADS = 8
RG_HEAD_DIM = D_RNN // RG_HEADS
CONV_W = 4
RG_C = 8.0
D_G = 2 * D_MODEL
GM_GROUPS = 8
GM_GROUP_DIM = D_G // GM_GROUPS
CHUNK = 128
N_RG = (DEPTH + 1) // 2
N_GM = DEPTH // 2
EPS = 1e-6

kernel_name = "hybrid_rglru_gmlp_macaron_trunk"


def rmsnorm(x, g):
    xf = x.astype(jnp.float32)
    y = xf * lax.rsqrt(jnp.mean(xf * xf, axis=-1, keepdims=True) + EPS)
    return (y * g.astype(jnp.float32)).astype(x.dtype)


def layernorm(x, g, b):
    xf = x.astype(jnp.float32)
    mu = jnp.mean(xf, axis=-1, keepdims=True)
    var = jnp.mean(jnp.square(xf - mu), axis=-1, keepdims=True)
    y = (xf - mu) * lax.rsqrt(var + EPS)
    return (y * g.astype(jnp.float32) + b.astype(jnp.float32)).astype(x.dtype)


def swiglu(h, w_in, w_out):
    gate, up = jnp.split(h @ w_in, 2, axis=-1)
    return (jax.nn.silu(gate) * up) @ w_out


def causal_depthwise_conv(x, w, b):
    S = x.shape[1]
    xp = jnp.pad(x, ((0, 0), (CONV_W - 1, 0), (0, 0)))
    y = b
    for k in range(CONV_W):
        y = y + xp[:, k:k + S, :] * w[k]
    return y


def _lin_combine(left, right):
    a1, b1 = left
    a2, b2 = right
    return a1 * a2, a2 * b1 + b2


def rglru_mixer(h, w_in, conv_w, conv_b, w_a, b_a, w_x, b_x, lam, w_out):
    B, S, _ = h.shape
    xb, gb = jnp.split(h @ w_in, 2, axis=-1)
    xb = causal_depthwise_conv(xb, conv_w, conv_b)
    xh = xb.reshape(B, S, RG_HEADS, RG_HEAD_DIM)
    r = jax.nn.sigmoid(jnp.einsum('bshd,hde->bshe', xh, w_a).reshape(B, S, D_RNN) + b_a)
    i = jax.nn.sigmoid(jnp.einsum('bshd,hde->bshe', xh, w_x).reshape(B, S, D_RNN) + b_x)
    log_a = -RG_C * r.astype(jnp.float32) * jax.nn.softplus(-lam.astype(jnp.float32))
    a = jnp.exp(log_a)
    mult = jnp.sqrt(-jnp.expm1(2.0 * log_a))
    bt = mult * (i * xb).astype(jnp.float32)
    _, hs = lax.associative_scan(_lin_combine, (a, bt), axis=1)
    y = hs.astype(h.dtype) * jax.nn.gelu(gb)
    return y @ w_out


def gmlp_mixer(h, w_in, ln_g, ln_b, w_s, b_s, w_out):
    B, S, _ = h.shape
    u, v = jnp.split(jax.nn.gelu(h @ w_in), 2, axis=-1)
    v = layernorm(v, ln_g, ln_b)
    vc = v.reshape(B, S // CHUNK, CHUNK, GM_GROUPS, GM_GROUP_DIM)
    mask = jnp.tril(jnp.ones((CHUNK, CHUNK), dtype=w_s.dtype))
    ws = w_s * mask
    vm = jnp.einsum('gts,bcsgd->bctgd', ws, vc) + b_s.T[None, None, :, :, None]
    y = u * vm.reshape(B, S, D_G)
    return y @ w_out


def setup_inputs(seed: int = 0) -> dict:
    key = jax.random.key(seed)
    ks = jax.random.split(key, 32)
    nrm = lambda k, shape, s: jax.random.normal(k, shape, jnp.float32) * s
    D = D_MODEL
    u_lam = jax.random.uniform(ks[20], (N_RG, D_RNN), jnp.float32, 0.9, 0.999)
    return {
        "x": nrm(ks[0], (BATCH, SEQ, D), 1.0),
        "c": nrm(ks[1], (BATCH, D), 1.0),
        "ada_w": nrm(ks[2], (DEPTH, D, N_SUB * 3 * D), 0.5 * D ** -0.5),
        "ada_b": nrm(ks[3], (DEPTH, N_SUB * 3 * D), 0.01),
        "norm_pre": 1.0 + nrm(ks[4], (DEPTH, N_SUB, D), 0.05),
        "norm_post": 1.0 + nrm(ks[5], (DEPTH, N_SUB, D), 0.05),
        "ffn_w_in": nrm(ks[6], (DEPTH, 2, D, 2 * D_FF), D ** -0.5),
        "ffn_w_out": nrm(ks[7], (DEPTH, 2, D_FF, D), D_FF ** -0.5),
        "rg_w_in": nrm(ks[8], (N_RG, D, 2 * D_RNN), D ** -0.5),
        "rg_conv_w": nrm(ks[9], (N_RG, CONV_W, D_RNN), CONV_W ** -0.5),
        "rg_conv_b": nrm(ks[10], (N_RG, D_RNN), 0.01),
        "rg_w_a": nrm(ks[11], (N_RG, RG_HEADS, RG_HEAD_DIM, RG_HEAD_DIM), RG_HEAD_DIM ** -0.5),
        "rg_b_a": nrm(ks[12], (N_RG, D_RNN), 0.01),
        "rg_w_x": nrm(ks[13], (N_RG, RG_HEADS, RG_HEAD_DIM, RG_HEAD_DIM), RG_HEAD_DIM ** -0.5),
        "rg_b_x": nrm(ks[14], (N_RG, D_RNN), 0.01),
        "rg_lambda": jnp.log(u_lam) - jnp.log1p(-u_lam),
        "rg_w_out": nrm(ks[15], (N_RG, D_RNN, D), D_RNN ** -0.5),
        "gm_w_in": nrm(ks[16], (N_GM, D, 2 * D_G), D ** -0.5),
        "gm_ln_g": 1.0 + nrm(ks[17], (N_GM, D_G), 0.05),
        "gm_ln_b": nrm(ks[18], (N_GM, D_G), 0.01),
        "gm_w_s": nrm(ks[19], (N_GM, GM_GROUPS, CHUNK, CHUNK), CHUNK ** -0.5),
        "gm_b_s": 1.0 + nrm(ks[21], (N_GM, GM_GROUPS, CHUNK), 0.1),
        "gm_w_out": nrm(ks[22], (N_GM, D_G, D), D_G ** -0.5),
    }


def reference(x, c, ada_w, ada_b, norm_pre, norm_post, ffn_w_in, ffn_w_out,
              rg_w_in, rg_conv_w, rg_conv_b, rg_w_a, rg_b_a, rg_w_x, rg_b_x,
              rg_lambda, rg_w_out, gm_w_in, gm_ln_g, gm_ln_b, gm_w_s, gm_b_s,
              gm_w_out):
    B = x.shape[0]
    c_act = jax.nn.silu(c)

    def sublayer(x, l, j, mod, fn, res_w):
        shift, scale, gate = mod[:, j, 0], mod[:, j, 1], mod[:, j, 2]
        h = rmsnorm(x, norm_pre[l, j]) * (1.0 + scale[:, None, :]) + shift[:, None, :]
        o = rmsnorm(fn(h), norm_post[l, j])
        return x + res_w * gate[:, None, :] * o

    for l in range(DEPTH):
        mod = (c_act @ ada_w[l] + ada_b[l]).reshape(B, N_SUB, 3, D_MODEL)
        x = sublayer(x, l, 0, mod, lambda h: swiglu(h, ffn_w_in[l, 0], ffn_w_out[l, 0]), 0.5)
        if l % N_MIXERS == 0:
            k = l // 2
            mix = lambda h: rglru_mixer(h, rg_w_in[k], rg_conv_w[k], rg_conv_b[k],
                                        rg_w_a[k], rg_b_a[k], rg_w_x[k], rg_b_x[k],
                                        rg_lambda[k], rg_w_out[k])
        else:
            k = l // 2
            mix = lambda h: gmlp_mixer(h, gm_w_in[k], gm_ln_g[k], gm_ln_b[k],
                                       gm_w_s[k], gm_b_s[k], gm_w_out[k])
        x = sublayer(x, l, 1, mod, mix, 1.0)
        x = sublayer(x, l, 2, mod, lambda h: swiglu(h, ffn_w_in[l, 1], ffn_w_out[l, 1]), 0.5)
    return x
```

```python
import functools

import jax
import jax.numpy as jnp
from jax import lax
from jax.experimental import pallas as pl
from jax.experimental.pallas import tpu as pltpu

EPS = 1e-6
RG_C = 8.0
N_SUB = 3

V7X_LANES = 128
V7X_SUBLANES = 8
V7X_MXU_DIM = 256
V7X_VMEM_LIMIT_BYTES = 56 * 1024 * 1024

F32 = jnp.float32
BF16 = jnp.bfloat16


def _dot(a, b):
    return jnp.dot(a, b, preferred_element_type=F32)


def _resident(block_shape, index_map):
    return pl.BlockSpec(block_shape, index_map, pipeline_mode=pl.Buffered(1))


def _pre_norm(x, mod_ref, npre_ref, j):
    g = npre_ref[j:j + 1, :]
    shift = mod_ref[3 * j:3 * j + 1, :]
    scale = mod_ref[3 * j + 1:3 * j + 2, :]
    y = x * lax.rsqrt(jnp.mean(x * x, axis=-1, keepdims=True) + EPS)
    return (y * g) * (1.0 + scale) + shift


def _post_norm_residual(x, f, mod_ref, npost_ref, j, res_w):
    g = npost_ref[j:j + 1, :]
    gate = mod_ref[3 * j + 2:3 * j + 3, :]
    o = (f * lax.rsqrt(jnp.mean(f * f, axis=-1, keepdims=True) + EPS)) * g
    return x + (res_w * gate) * o


def _mod_kernel(c_ref, w_ref, b_ref, o_ref):
    c = c_ref[...]
    c_act = c * jax.nn.sigmoid(c)
    o_ref[...] = jnp.dot(c_act, w_ref[...], preferred_element_type=F32,
                         precision=lax.Precision.HIGHEST) + b_ref[...]


def _modulation(c, ada_w, ada_b):
    depth, d, n = ada_w.shape
    b = c.shape[0]
    tn = n // 4 if n % (4 * V7X_LANES) == 0 else n
    return pl.pallas_call(
        _mod_kernel,
        grid=(depth, n // tn),
        in_specs=[
            pl.BlockSpec((b, d), lambda l, i: (0, 0)),
            pl.BlockSpec((None, d, tn), lambda l, i: (l, 0, i)),
            pl.BlockSpec((None, 1, tn), lambda l, i: (l, 0, i)),
        ],
        out_specs=pl.BlockSpec((None, b, tn), lambda l, i: (l, 0, i)),
        out_shape=jax.ShapeDtypeStruct((depth, b, n), F32),
        compiler_params=pltpu.CompilerParams(
            dimension_semantics=("arbitrary", "arbitrary"),
            vmem_limit_bytes=V7X_VMEM_LIMIT_BYTES),
        name="adaln_mod",
    )(c, ada_w, ada_b.reshape(depth, 1, n))


def _ffn_kernel(x_ref, mod_ref, npre_ref, npost_ref, win_ref, wout_ref, o_ref,
                *, j, d_ff, chunks):
    x = x_ref[...]
    h = _pre_norm(x, mod_ref, npre_ref, j).astype(BF16)
    acc = None
    for c0, c1 in chunks:
        gate = _dot(h, win_ref[:, c0:c1])
        up = _dot(h, win_ref[:, d_ff + c0:d_ff + c1])
        act = ((gate * jax.nn.sigmoid(gate)) * up).astype(BF16)
        part = _dot(act, wout_ref[c0:c1, :])
        acc = part if acc is None else acc + part
    o_ref[...] = _post_norm_residual(x, acc, mod_ref, npost_ref, j, 0.5)


def _ffn_chunks(d_ff):
    step = 3 * V7X_MXU_DIM
    if d_ff % V7X_MXU_DIM != 0 or d_ff <= step:
        return ((0, d_ff),)
    edges = list(range(0, d_ff, step)) + [d_ff]
    return tuple(zip(edges[:-1], edges[1:]))


def _ffn_sublayer(x2, mod, npre, npost, w_in, w_out, *, l, j, jj, seq, tm):
    t, d = x2.shape
    d_ff = w_out.shape[2]
    tiles_per_seq = seq // tm
    kern = functools.partial(_ffn_kernel, j=j, d_ff=d_ff, chunks=_ffn_chunks(d_ff))
    return pl.pallas_call(
        kern,
        grid=(t // tm,),
        in_specs=[
            pl.BlockSpec((tm, d), lambda i: (i, 0)),
            pl.BlockSpec((None, None, 3 * N_SUB, d),
                         lambda i: (l, i // tiles_per_seq, 0, 0)),
            pl.BlockSpec((None, N_SUB, d), lambda i: (l, 0, 0)),
            pl.BlockSpec((None, N_SUB, d), lambda i: (l, 0, 0)),
            _resident((None, None, d, 2 * d_ff), lambda i: (l, jj, 0, 0)),
            _resident((None, None, d_ff, d), lambda i: (l, jj, 0, 0)),
        ],
        out_specs=pl.BlockSpec((tm, d), lambda i: (i, 0)),
        out_shape=jax.ShapeDtypeStruct((t, d), F32),
        compiler_params=pltpu.CompilerParams(
            dimension_semantics=("arbitrary",),
            vmem_limit_bytes=V7X_VMEM_LIMIT_BYTES),
        name=f"ffn_l{l}_{jj}",
    )(x2, mod, npre, npost, w_in, w_out)


def _rg_kernel(x_ref, mod_ref, npre_ref, npost_ref, win_ref, cw_ref, cb_ref,
               wg_ref, ba_ref, bx_ref, lam_ref, wout_ref, o_ref,
               xbuf, abuf, bbuf, hbuf, hcar, *, j, ts, d_rnn, heads, conv_w):
    hd = d_rnn // heads
    pad = V7X_SUBLANES

    @pl.when(pl.program_id(1) == 0)
    def _():
        xbuf[0:pad, :] = jnp.zeros((pad, d_rnn), F32)
        hcar[...] = jnp.zeros((V7X_SUBLANES, d_rnn), F32)

    x = x_ref[...]
    h = _pre_norm(x, mod_ref, npre_ref, j).astype(BF16)
    xg = _dot(h, win_ref[...])
    xbuf[pad:pad + ts, :] = xg[:, :d_rnn]
    gb = xg[:, d_rnn:]

    conv = cb_ref[...]
    for k in range(conv_w):
        off = pad - (conv_w - 1) + k
        conv = conv + xbuf[off:off + ts, :] * cw_ref[k:k + 1, :]
    xbuf[0:pad, :] = xbuf[ts:ts + pad, :]

    z = -lam_ref[...]
    sp = jnp.maximum(z, 0.0) + jnp.log1p(jnp.exp(-jnp.abs(z)))
    conv_b = conv.astype(BF16)
    for hh in range(heads):
        cs = slice(hh * hd, (hh + 1) * hd)
        pre = _dot(conv_b[:, cs], wg_ref[hh])
        r = jax.nn.sigmoid(pre[:, :hd] + ba_ref[:, cs])
        i = jax.nn.sigmoid(pre[:, hd:] + bx_ref[:, cs])
        log_a = (-RG_C * r) * sp[:, cs]
        a = jnp.exp(log_a)
        mult = jnp.sqrt(jnp.tanh(-log_a) * (a * a + 1.0))
        abuf[:, cs] = a
        bbuf[:, cs] = mult * (i * conv[:, cs])

    row = lax.broadcasted_iota(jnp.int32, (V7X_SUBLANES, d_rnn), 0)

    def group(gi, carry):
        r0 = pl.multiple_of(gi * V7X_SUBLANES, V7X_SUBLANES)
        a = abuf[pl.ds(r0, V7X_SUBLANES), :]
        b = bbuf[pl.ds(r0, V7X_SUBLANES), :]
        sh = 1
        while sh < V7X_SUBLANES:
            keep = row >= sh
            b = jnp.where(keep, a * pltpu.roll(b, sh, axis=0) + b, b)
            a = jnp.where(keep, a * pltpu.roll(a, sh, axis=0), a)
            sh *= 2
        hs = a * carry + b
        hbuf[pl.ds(r0, V7X_SUBLANES), :] = hs
        last = hs[V7X_SUBLANES - 1:V7X_SUBLANES, :]
        return jnp.broadcast_to(last, (V7X_SUBLANES, d_rnn))

    hcar[...] = lax.fori_loop(0, ts // V7X_SUBLANES, group, hcar[...])

    y = (hbuf[...] * jax.nn.gelu(gb)).astype(BF16)
    f = _dot(y, wout_ref[...])
    o_ref[...] = _post_norm_residual(x, f, mod_ref, npost_ref, j, 1.0)


def _rg_sublayer(x2, mod, npre, npost, w_in, conv_w, conv_b, w_gate, b_a, b_x,
                 lam, w_out, *, l, k, j, batch, seq, ts):
    t, d = x2.shape
    d_rnn = w_out.shape[1]
    heads, hd = w_gate.shape[1], w_gate.shape[2]
    cw = conv_w.shape[1]
    nt = seq // ts
    kern = functools.partial(_rg_kernel, j=j, ts=ts, d_rnn=d_rnn, heads=heads, conv_w=cw)
    vec = lambda: _resident((None, 1, d_rnn), lambda b, s: (k, 0, 0))
    return pl.pallas_call(
        kern,
        grid=(batch, nt),
        in_specs=[
            pl.BlockSpec((ts, d), lambda b, s: (b * nt + s, 0)),
            pl.BlockSpec((None, None, 3 * N_SUB, d), lambda b, s: (l, b, 0, 0)),
            pl.BlockSpec((None, N_SUB, d), lambda b, s: (l, 0, 0)),
            pl.BlockSpec((None, N_SUB, d), lambda b, s: (l, 0, 0)),
            _resident((None, d, 2 * d_rnn), lambda b, s: (k, 0, 0)),
            _resident((None, cw, d_rnn), lambda b, s: (k, 0, 0)),
            vec(),
            _resident((None, heads, hd, 2 * hd), lambda b, s: (k, 0, 0, 0)),
            vec(), vec(), vec(),
            _resident((None, d_rnn, d), lambda b, s: (k, 0, 0)),
        ],
        out_specs=pl.BlockSpec((ts, d), lambda b, s: (b * nt + s, 0)),
        out_shape=jax.ShapeDtypeStruct((t, d), F32),
        scratch_shapes=[
            pltpu.VMEM((ts + V7X_SUBLANES, d_rnn), F32),
            pltpu.VMEM((ts, d_rnn), F32),
            pltpu.VMEM((ts, d_rnn), F32),
            pltpu.VMEM((ts, d_rnn), F32),
            pltpu.VMEM((V7X_SUBLANES, d_rnn), F32),
        ],
        compiler_params=pltpu.CompilerParams(
            dimension_semantics=("arbitrary", "arbitrary"),
            vmem_limit_bytes=V7X_VMEM_LIMIT_BYTES),
        name=f"rglru_l{l}",
    )(x2, mod, npre, npost, w_in, conv_w, conv_b.reshape(-1, 1, d_rnn), w_gate,
      b_a.reshape(-1, 1, d_rnn), b_x.reshape(-1, 1, d_rnn),
      lam.reshape(-1, 1, d_rnn), w_out)


def _gm_kernel(x_ref, mod_ref, npre_ref, npost_ref, win_ref, lng_ref, lnb_ref,
               ws_ref, bs_ref, wout_ref, o_ref, ybuf, *, j, tm, d_g, groups, chunk):
    gd = d_g // groups
    x = x_ref[...]
    h = _pre_norm(x, mod_ref, npre_ref, j).astype(BF16)
    uv = jax.nn.gelu(_dot(h, win_ref[...]))
    u = uv[:, :d_g]
    v = uv[:, d_g:]
    mu = jnp.mean(v, axis=-1, keepdims=True)
    vc = v - mu
    var = jnp.mean(vc * vc, axis=-1, keepdims=True)
    vn = ((vc * lax.rsqrt(var + EPS)) * lng_ref[...] + lnb_ref[...]).astype(BF16)

    tri = (lax.broadcasted_iota(jnp.int32, (chunk, chunk), 0)
           >= lax.broadcasted_iota(jnp.int32, (chunk, chunk), 1)).astype(F32)
    for g in range(groups):
        ws = (ws_ref[g] * tri).astype(BF16)
        bias = bs_ref[:, g:g + 1]
        cs = slice(g * gd, (g + 1) * gd)
        for c in range(tm // chunk):
            rs = slice(c * chunk, (c + 1) * chunk)
            vm = _dot(ws, vn[rs, cs]) + bias
            ybuf[rs, cs] = (u[rs, cs] * vm).astype(BF16)

    f = _dot(ybuf[...], wout_ref[...])
    o_ref[...] = _post_norm_residual(x, f, mod_ref, npost_ref, j, 1.0)


def _gm_sublayer(x2, mod, npre, npost, w_in, ln_g, ln_b, w_s, b_s_t, w_out,
                 *, l, k, j, seq, tm):
    t, d = x2.shape
    d_g = w_out.shape[1]
    groups, chunk = w_s.shape[1], w_s.shape[2]
    tiles_per_seq = seq // tm
    kern = functools.partial(_gm_kernel, j=j, tm=tm, d_g=d_g, groups=groups, chunk=chunk)
    return pl.pallas_call(
        kern,
        grid=(t // tm,),
        in_specs=[
            pl.BlockSpec((tm, d), lambda i: (i, 0)),
            pl.BlockSpec((None, None, 3 * N_SUB, d),
                         lambda i: (l, i // tiles_per_seq, 0, 0)),
            pl.BlockSpec((None, N_SUB, d), lambda i: (l, 0, 0)),
            pl.BlockSpec((None, N_SUB, d), lambda i: (l, 0, 0)),
            _resident((None, d, 2 * d_g), lambda i: (k, 0, 0)),
            _resident((None, 1, d_g), lambda i: (k, 0, 0)),
            _resident((None, 1, d_g), lambda i: (k, 0, 0)),
            _resident((None, groups, chunk, chunk), lambda i: (k, 0, 0, 0)),
            _resident((None, chunk, groups), lambda i: (k, 0, 0)),
            _resident((None, d_g, d), lambda i: (k, 0, 0)),
        ],
        out_specs=pl.BlockSpec((tm, d), lambda i: (i, 0)),
        out_shape=jax.ShapeDtypeStruct((t, d), F32),
        scratch_shapes=[pltpu.VMEM((tm, d_g), BF16)],
        compiler_params=pltpu.CompilerParams(
            dimension_semantics=("arbitrary",),
            vmem_limit_bytes=V7X_VMEM_LIMIT_BYTES),
        name=f"gmlp_l{l}",
    )(x2, mod, npre, npost, w_in, ln_g.reshape(-1, 1, d_g), ln_b.reshape(-1, 1, d_g),
      w_s, b_s_t, w_out)


def _row_tile(seq, target, multiple):
    best = None
    for cand in range(multiple, min(seq, target) + 1, multiple):
        if seq % cand == 0:
            best = cand
    assert best is not None, (seq, target, multiple)
    return best


def kernel(x, c, ada_w, ada_b, norm_pre, norm_post, ffn_w_in, ffn_w_out, rg_w_in, rg_conv_w, rg_conv_b, rg_w_a, rg_b_a, rg_w_x, rg_b_x, rg_lambda, rg_w_out, gm_w_in, gm_ln_g, gm_ln_b, gm_w_s, gm_b_s, gm_w_out):
    batch, seq, d = x.shape
    depth = ada_w.shape[0]
    chunk = gm_w_s.shape[2]

    mod = _modulation(c, ada_w, ada_b).reshape(depth, batch, 3 * N_SUB, d)

    ffn_w_in_b = ffn_w_in.astype(BF16)
    ffn_w_out_b = ffn_w_out.astype(BF16)
    rg_w_in_b = rg_w_in.astype(BF16)
    rg_w_out_b = rg_w_out.astype(BF16)
    rg_w_gate_b = jnp.concatenate([rg_w_a, rg_w_x], axis=-1).astype(BF16)
    gm_w_in_b = gm_w_in.astype(BF16)
    gm_w_out_b = gm_w_out.astype(BF16)
    gm_b_s_t = jnp.swapaxes(gm_b_s, 1, 2)

    tm_ffn = _row_tile(seq, 512, V7X_SUBLANES)
    ts_rg = _row_tile(seq, 512, V7X_SUBLANES)
    tm_gm = _row_tile(seq, 256, chunk)

    x2 = x.reshape(batch * seq, d)
    for l in range(depth):
        k = l // 2
        x2 = _ffn_sublayer(x2, mod, norm_pre, norm_post, ffn_w_in_b, ffn_w_out_b,
                           l=l, j=0, jj=0, seq=seq, tm=tm_ffn)
        if l % 2 == 0:
            x2 = _rg_sublayer(x2, mod, norm_pre, norm_post, rg_w_in_b, rg_conv_w,
                              rg_conv_b, rg_w_gate_b, rg_b_a, rg_b_x, rg_lambda,
                              rg_w_out_b, l=l, k=k, j=1, batch=batch, seq=seq, ts=ts_rg)
        else:
            x2 = _gm_sublayer(x2, mod, norm_pre, norm_post, gm_w_in_b, gm_ln_g,
                              gm_ln_b, gm_w_s, gm_b_s_t, gm_w_out_b,
                              l=l, k=k, j=1, seq=seq, tm=tm_gm)
        x2 = _ffn_sublayer(x2, mod, norm_pre, norm_post, ffn_w_in_b, ffn_w_out_b,
                           l=l, j=2, jj=1, seq=seq, tm=tm_ffn)
    return x2.reshape(batch, seq, d)
```
